```python
import math
import jax, jax.numpy as jnp
from jax import lax
import numpy as np

D_MODEL = 1024
BATCH = 16
SEQ = 2048
DEPTH = 2

GRID_W = 64
CTX_LEN = 256
N_MIXERS = 2
MIXER_ATTN = 0
MIXER_POOL = 1
N_HEADS = 8
HEAD_DIM = 64
V_DIM = 2 * HEAD_DIM
QK_WIDTH = N_HEADS * 2 * HEAD_DIM
V_WIDTH = N_HEADS * V_DIM
ROPE_BASE = 10000.0
ROPE_PAIRS_PER_AXIS = HEAD_DIM // 4
POOL_WINDOWS = (2, 4, 8, 16)
N_POOL_GROUPS = len(POOL_WINDOWS)
POOL_GROUP = D_MODEL // N_POOL_GROUPS
D_FF = -(-8 * D_MODEL // (3 * 256)) * 256
Q_BLOCK = 128
ALPHA = (2.0 * DEPTH) ** 0.25
BETA = (8.0 * DEPTH) ** -0.25
LN_EPS = 1e-5
N_MOD = 6
N_ATTN_LAYERS = (DEPTH + 1) // 2
N_POOL_LAYERS = DEPTH // 2

kernel_name = "hybrid_diffattn_pool_dit_backbone"


def layer_norm(x, g, b):
    xf = x.astype(jnp.float32)
    mu = jnp.mean(xf, axis=-1, keepdims=True)
    var = jnp.mean(jnp.square(xf - mu), axis=-1, keepdims=True)
    return ((xf - mu) * lax.rsqrt(var + LN_EPS) * g + b).astype(x.dtype)


def adaln_params(cond, w_mod, b_mod):
    m = jax.nn.silu(cond) @ w_mod + b_mod
    return jnp.split(m, N_MOD, axis=-1)


def modulate(h, shift, scale):
    return h * (1 + scale) + shift


def axial_rope_tables(rows, dtype):
    row = jnp.repeat(jnp.arange(rows, dtype=jnp.float32), GRID_W)
    col = jnp.tile(jnp.arange(GRID_W, dtype=jnp.float32), rows)
    inv = jnp.power(ROPE_BASE, -jnp.arange(ROPE_PAIRS_PER_AXIS, dtype=jnp.float32) / ROPE_PAIRS_PER_AXIS)
    ang = jnp.concatenate([row[:, None] * inv, col[:, None] * inv], axis=-1)
    cos = jnp.cos(ang)[:, None, None, :].astype(dtype)
    sin = jnp.sin(ang)[:, None, None, :].astype(dtype)
    return cos, sin


def apply_rope(x, cos, sin):
    half = HEAD_DIM // 2
    x1, x2 = x[..., :half], x[..., half:]
    return jnp.concatenate([x1 * cos - x2 * sin, x1 * sin + x2 * cos], axis=-1)


def project_q(h, w_qkv):
    B, S = h.shape[:2]
    return (h @ w_qkv[:, :QK_WIDTH]).reshape(B, S, N_HEADS, 2, HEAD_DIM)


def project_kv(h, w_qkv):
    B, S = h.shape[:2]
    kv = h @ w_qkv[:, QK_WIDTH:]
    k = kv[..., :QK_WIDTH].reshape(B, S, N_HEADS, 2, HEAD_DIM)
    v = kv[..., QK_WIDTH:].reshape(B, S, N_HEADS, V_DIM)
    return k, v


def diff_softmax_mix(q, k, v, lam):
    s = jnp.einsum('bqhmd,bkhmd->bhmqk', q * (HEAD_DIM ** -0.5), k).astype(jnp.float32)
    p = jax.nn.softmax(s, axis=-1)
    a = p[:, :, 0] - lam * p[:, :, 1]
    return jnp.einsum('bhqk,bkhe->bqhe', a.astype(v.dtype), v)


def diff_head_out(o, subln_g, lam_init, w_o):
    B, Q = o.shape[:2]
    of = o.astype(jnp.float32)
    of = of * lax.rsqrt(jnp.mean(of * of, axis=-1, keepdims=True) + LN_EPS) * subln_g * (1.0 - lam_init)
    return of.astype(o.dtype).reshape(B, Q, V_WIDTH) @ w_o


def latent_diff_attention(q_l, k_all, v_all, lam):
    B, S = q_l.shape[:2]
    n_blk = S // Q_BLOCK
    qb = jnp.moveaxis(q_l.reshape(B, n_blk, Q_BLOCK, N_HEADS, 2, HEAD_DIM), 1, 0)
    ob = lax.map(lambda q: diff_softmax_mix(q, k_all, v_all, lam), qb)
    return jnp.moveaxis(ob, 0, 1).reshape(B, S, N_HEADS, V_DIM)


def multiscale_pool(h, w_pool, pool_scale):
    B, S, D = h.shape
    hf = h.astype(jnp.float32)
    cs = jnp.concatenate([jnp.zeros((B, 1, D), jnp.float32), lax.cumsum(hf, axis=1)], axis=1)
    t = jnp.arange(S)
    outs = []
    for g, w in enumerate(POOL_WINDOWS):
        lo = jnp.clip(t - w // 2, 0, S)
        hi = jnp.clip(t + w - w // 2, 0, S)
        cg = cs[..., g * POOL_GROUP:(g + 1) * POOL_GROUP]
        mean = (cg[:, hi] - cg[:, lo]) / (hi - lo).astype(jnp.float32)[None, :, None]
        outs.append(mean - hf[..., g * POOL_GROUP:(g + 1) * POOL_GROUP])
    pooled = jnp.stack(outs, axis=2).astype(h.dtype)
    y = jnp.einsum('bsgi,gio->bsgo', pooled, w_pool).reshape(B, S, D)
    return y * pool_scale


def swiglu(h, w_in, w_out):
    g, u = jnp.split(h @ w_in, 2, axis=-1)
    return (jax.nn.silu(g) * u) @ w_out


def reads_context(layer):
    return (layer % N_MIXERS) == MIXER_ATTN


def setup_inputs(seed: int = 0) -> dict:
    key = jax.random.key(seed)
    ks = jax.random.split(key, 16)
    D = D_MODEL
    f32 = jnp.float32
    return {
        "x": jax.random.normal(ks[0], (BATCH, SEQ, D), f32),
        "c": jax.random.normal(ks[1], (BATCH, D), f32),
        "ctx": jax.random.normal(ks[2], (BATCH, CTX_LEN, D), f32),
        "c_ctx": jax.random.normal(ks[3], (D,), f32),
        "w_mod": jax.random.normal(ks[4], (DEPTH, D, N_MOD * D), f32) * (0.5 * D ** -0.5),
        "b_mod": jax.random.normal(ks[5], (DEPTH, N_MOD * D), f32) * 0.01,
        "ln_g": 1.0 + 0.02 * jax.random.normal(ks[6], (DEPTH, 2, D), f32),
        "ln_b": 0.02 * jax.random.normal(ks[7], (DEPTH, 2, D), f32),
        "attn_w_qkv": jax.random.normal(ks[8], (N_ATTN_LAYERS, D, QK_WIDTH + QK_WIDTH + V_WIDTH), f32) * D ** -0.5,
        "attn_lambda": 0.1 * jax.random.normal(ks[9], (N_ATTN_LAYERS, 4, HEAD_DIM), f32),
        "attn_subln_g": 1.0 + 0.02 * jax.random.normal(ks[10], (N_ATTN_LAYERS, V_DIM), f32),
        "attn_w_o": jax.random.normal(ks[11], (N_ATTN_LAYERS, V_WIDTH, D), f32) * (V_WIDTH ** -0.5 * BETA),
        "pool_w": jax.random.normal(ks[12], (N_POOL_LAYERS, N_POOL_GROUPS, POOL_GROUP, POOL_GROUP), f32) * (POOL_GROUP ** -0.5 * BETA),
        "pool_scale": 0.5 + 0.05 * jax.random.normal(ks[13], (N_POOL_LAYERS, D), f32),
        "ffn_w_in": jax.random.normal(ks[14], (DEPTH, D, 2 * D_FF), f32) * D ** -0.5,
        "ffn_w_out": jax.random.normal(ks[15], (DEPTH, D_FF, D), f32) * (D_FF ** -0.5 * BETA),
    }


def reference(x, c, ctx, c_ctx, w_mod, b_mod, ln_g, ln_b, attn_w_qkv, attn_lambda, attn_subln_g,
              attn_w_o, pool_w, pool_scale, ffn_w_in, ffn_w_out):
    n_lat = x.shape[1]
    rows = n_lat // GRID_W
    cos, sin = axial_rope_tables(rows, x.dtype)
    h_ctx = ctx
    for i in range(DEPTH):
        mixer = i % N_MIXERS
        ctx_later = any(reads_context(j) for j in range(i + 1, DEPTH))
        sh1, sc1, g1, sh2, sc2, g2 = adaln_params(c[:, None, :], w_mod[i], b_mod[i])
        csh1, csc1, cg1, csh2, csc2, cg2 = adaln_params(c_ctx, w_mod[i], b_mod[i])
        hx = modulate(x, sh1, sc1)
        hc = modulate(h_ctx, csh1, csc1)
        if mixer == MIXER_ATTN:
            a = i // N_MIXERS
            lam_init = 0.8 - 0.6 * math.exp(-0.3 * i)
            lp = attn_lambda[a].astype(jnp.float32)
            lam = jnp.exp(jnp.sum(lp[0] * lp[1])) - jnp.exp(jnp.sum(lp[2] * lp[3])) + lam_init
            w_qkv = attn_w_qkv[a]
            k_c, v_c = project_kv(hc, w_qkv)
            q_l = apply_rope(project_q(hx, w_qkv), cos, sin)
            k_l, v_l = project_kv(hx, w_qkv)
            k_l = apply_rope(k_l, cos, sin)
            k_all = jnp.concatenate([k_c, k_l], axis=1)
            v_all = jnp.concatenate([v_c, v_l], axis=1)
            y_lat = diff_head_out(latent_diff_attention(q_l, k_all, v_all, lam),
                                  attn_subln_g[a], lam_init, attn_w_o[a])
            if ctx_later:
                q_c = project_q(hc, w_qkv)
                y_ctx = diff_head_out(diff_softmax_mix(q_c, k_c, v_c, lam),
                                      attn_subln_g[a], lam_init, attn_w_o[a])
        else:
            p = i // N_MIXERS
            y_lat = multiscale_pool(hx, pool_w[p], pool_scale[p])
            if ctx_later:
                y_ctx = multiscale_pool(hc, pool_w[p], pool_scale[p])
        x = layer_norm(ALPHA * x + g1 * y_lat, ln_g[i, 0], ln_b[i, 0])
        x = layer_norm(ALPHA * x + g2 * swiglu(modulate(x, sh2, sc2), ffn_w_in[i], ffn_w_out[i]),
                       ln_g[i, 1], ln_b[i, 1])
        if ctx_later:
            h_ctx = layer_norm(ALPHA * h_ctx + cg1 * y_ctx, ln_g[i, 0], ln_b[i, 0])
            h_ctx = layer_norm(ALPHA * h_ctx + cg2 * swiglu(modulate(h_ctx, csh2, csc2), ffn_w_in[i], ffn_w_out[i]),
                               ln_g[i, 1], ln_b[i, 1])
    return x
```

```python
import functools
import math

import jax
import jax.numpy as jnp
from jax import lax
from jax.experimental import pallas as pl
from jax.experimental.pallas import tpu as pltpu

D_MODEL = 1024
SEQ = 2048
DEPTH = 2
GRID_W = 64
CTX_LEN = 256
N_HEADS = 8
HEAD_DIM = 64
HEAD_WIDTH = 2 * HEAD_DIM
ROPE_BASE = 10000.0
ROPE_PAIRS_PER_AXIS = HEAD_DIM // 4
POOL_WINDOWS = (2, 4, 8, 16)
POOL_GROUP = D_MODEL // len(POOL_WINDOWS)
POOL_HALO = 8
D_FF = 2816
ALPHA = (2.0 * DEPTH) ** 0.25
LN_EPS = 1e-5
N_MOD = 6
LAM_INIT_0 = 0.8 - 0.6 * math.exp(-0.3 * 0)

COND_ROWS = 24
KV_LEN = CTX_LEN + SEQ

BF16 = jnp.bfloat16
F32 = jnp.float32

VMEM_LIMIT_BYTES = 56 * 1024 * 1024


def _dot(a, b):
    return jnp.dot(a, b, preferred_element_type=F32)


def _dot_nt(a, b):
    return lax.dot_general(a, b, (((1,), (1,)), ((), ())), preferred_element_type=F32)


def _sigmoid(v):
    return 1.0 / (1.0 + jnp.exp(-v))


def _layer_norm(z, g, b):
    mu = jnp.mean(z, axis=-1, keepdims=True)
    d = z - mu
    var = jnp.mean(d * d, axis=-1, keepdims=True)
    return d * lax.rsqrt(var + LN_EPS) * g + b


def _params(*semantics):
    return pltpu.CompilerParams(dimension_semantics=semantics, vmem_limit_bytes=VMEM_LIMIT_BYTES)


def _resident(shape):
    return pl.BlockSpec(shape, lambda *_: (0,) * len(shape), pipeline_mode=pl.Buffered(1))


MOD_TILE_N = 1024


def _mod_kernel(c_ref, w_ref, b_ref, o_ref):
    c = c_ref[...]
    s = (c * _sigmoid(c)).astype(BF16)
    o_ref[...] = _dot(s, w_ref[...].astype(BF16)) + b_ref[...]


def _modulation(cond, w_mod, b_mod):
    n = N_MOD * D_MODEL
    return pl.pallas_call(
        _mod_kernel,
        grid=(DEPTH, n // MOD_TILE_N),
        in_specs=[
            pl.BlockSpec((COND_ROWS, D_MODEL), lambda l, j: (0, 0)),
            pl.BlockSpec((None, D_MODEL, MOD_TILE_N), lambda l, j: (l, 0, j)),
            pl.BlockSpec((None, 1, MOD_TILE_N), lambda l, j: (l, 0, j)),
        ],
        out_specs=pl.BlockSpec((None, COND_ROWS, MOD_TILE_N), lambda l, j: (l, 0, j)),
        out_shape=jax.ShapeDtypeStruct((DEPTH, COND_ROWS, n), F32),
        compiler_params=_params("parallel", "parallel"),
        name="adaln_modulation",
    )(cond, w_mod, b_mod.reshape(DEPTH, 1, n))


def _mod_spec(layer, row_of, which):
    return pl.BlockSpec((None, None, 1, D_MODEL), lambda *ids: (layer, row_of(*ids), 0, which))


QKV_TILE = 256


def _qkv_kernel(x_ref, ctx_ref, sh_ref, sc_ref, csh_ref, csc_ref, cos_ref, sin_ref,
                wq_ref, wk_ref, wv_ref, q_ref, k_ref, v_ref):
    is_ctx = pl.program_id(1) == 0
    src = jnp.where(is_ctx, ctx_ref[...], x_ref[...])
    shift = jnp.where(is_ctx, csh_ref[...], sh_ref[...])
    scale = jnp.where(is_ctx, csc_ref[...], sc_ref[...])
    h = (src * (1.0 + scale) + shift).astype(BF16)
    cos = cos_ref[...]
    sin = sin_ref[...]
    q = _dot(h, wq_ref[...])
    k = _dot(h, wk_ref[...])
    v_ref[...] = _dot(h, wv_ref[...]).astype(BF16)
    for hd in range(N_HEADS):
        sl = slice(hd * HEAD_WIDTH, (hd + 1) * HEAD_WIDTH)
        qh = q[:, sl]
        kh = k[:, sl]
        q_ref[:, sl] = (qh * cos + pltpu.roll(qh, HEAD_DIM, 1) * sin).astype(BF16)
        kr = kh * cos + pltpu.roll(kh, HEAD_DIM, 1) * sin
        k_ref[:, sl] = jnp.where(is_ctx, kh, kr).astype(BF16)


def _qkv_project(x, ctx, m4, cos_t, sin_t, wq, wk, wv):
    batch = x.shape[0]
    n_tiles = KV_LEN // QKV_TILE
    ctx_tiles = CTX_LEN // QKV_TILE
    assert ctx_tiles == 1

    def lat(i):
        return jnp.maximum(i - ctx_tiles, 0)

    tile = lambda f: pl.BlockSpec((None, QKV_TILE, D_MODEL), f)
    return pl.pallas_call(
        _qkv_kernel,
        grid=(batch, n_tiles),
        in_specs=[
            tile(lambda b, i: (b, lat(i), 0)),
            tile(lambda b, i: (b, 0, 0)),
            _mod_spec(0, lambda b, i: b, 0),
            _mod_spec(0, lambda b, i: b, 1),
            _mod_spec(0, lambda b, i: batch, 0),
            _mod_spec(0, lambda b, i: batch, 1),
            pl.BlockSpec((QKV_TILE, HEAD_WIDTH), lambda b, i: (lat(i), 0)),
            pl.BlockSpec((QKV_TILE, HEAD_WIDTH), lambda b, i: (lat(i), 0)),
            _resident((D_MODEL, D_MODEL)),
            _resident((D_MODEL, D_MODEL)),
            _resident((D_MODEL, D_MODEL)),
        ],
        out_specs=[
            tile(lambda b, i: (b, lat(i), 0)),
            tile(lambda b, i: (b, i, 0)),
            tile(lambda b, i: (b, i, 0)),
        ],
        out_shape=[
            jax.ShapeDtypeStruct((batch, SEQ, D_MODEL), BF16),
            jax.ShapeDtypeStruct((batch, KV_LEN, D_MODEL), BF16),
            jax.ShapeDtypeStruct((batch, KV_LEN, D_MODEL), BF16),
        ],
        compiler_params=_params("parallel", "arbitrary"),
        name="qkv_rope",
    )(x, ctx, m4, m4, m4, m4, cos_t, sin_t, wq, wk, wv)


ATTN_Q_TILE = 256


def _attn_kernel(lam_ref, g_ref, q_ref, k_ref, v_ref, o_ref):
    lp = lam_ref[...]
    lam = (jnp.exp(jnp.sum(lp[0:1] * lp[1:2], axis=-1, keepdims=True))
           - jnp.exp(jnp.sum(lp[2:3] * lp[3:4], axis=-1, keepdims=True)) + LAM_INIT_0)
    out_gain = g_ref[...] * (1.0 - LAM_INIT_0)
    lane = lax.broadcasted_iota(jnp.int32, (1, HEAD_WIDTH), 1)
    first = jnp.where((lane // (HEAD_DIM // 2)) % 2 == 0, 1.0, 0.0)
    k = k_ref[...]
    v = v_ref[...]

    def softmax_parts(s):
        p = jnp.exp(s - jnp.max(s, axis=-1, keepdims=True))
        return p, jnp.sum(p, axis=-1, keepdims=True)

    def body(t, carry):
        rows = pl.ds(pl.multiple_of(t * ATTN_Q_TILE, ATTN_Q_TILE), ATTN_Q_TILE)
        q = q_ref[rows, :].astype(F32)
        p1, l1 = softmax_parts(_dot_nt((q * first).astype(BF16), k))
        p2, l2 = softmax_parts(_dot_nt((q * (1.0 - first)).astype(BF16), k))
        a = p1 * (1.0 / l1) - p2 * (lam / l2)
        o = _dot(a.astype(BF16), v)
        o = o * lax.rsqrt(jnp.mean(o * o, axis=-1, keepdims=True) + LN_EPS) * out_gain
        o_ref[rows, :] = o.astype(BF16)
        return carry

    lax.fori_loop(0, SEQ // ATTN_Q_TILE, body, 0)


def _attention(q, k_all, v_all, lam_params, subln_g):
    batch = q.shape[0]
    head_cols = lambda rows: pl.BlockSpec((None, rows, HEAD_WIDTH), lambda b, h: (b, 0, h))
    return pl.pallas_call(
        _attn_kernel,
        grid=(batch, N_HEADS),
        in_specs=[
            pl.BlockSpec((4, HEAD_DIM), lambda b, h: (0, 0)),
            pl.BlockSpec((1, HEAD_WIDTH), lambda b, h: (0, 0)),
            head_cols(SEQ),
            head_cols(KV_LEN),
            head_cols(KV_LEN),
        ],
        out_specs=head_cols(SEQ),
        out_shape=jax.ShapeDtypeStruct((batch, SEQ, D_MODEL), BF16),
        compiler_params=_params("parallel", "parallel"),
        name="diff_attention",
    )(lam_params, subln_g.reshape(1, HEAD_WIDTH), q, k_all, v_all)


POOL_TILE = 512


def _pool_kernel(x_ref, prev_ref, next_ref, sh_ref, sc_ref, o_ref, e_ref):
    i = pl.program_id(1)
    gain = 1.0 + sc_ref[...]
    shift = sh_ref[...]
    lo_rows = slice(0, POOL_HALO)
    mid_rows = slice(POOL_HALO, POOL_HALO + POOL_TILE)
    hi_rows = slice(POOL_HALO + POOL_TILE, POOL_HALO + POOL_TILE + POOL_HALO)
    e_ref[lo_rows, :] = jnp.where(i > 0, prev_ref[...] * gain + shift, 0.0)
    e_ref[mid_rows, :] = x_ref[...] * gain + shift
    e_ref[hi_rows, :] = jnp.where(i < pl.num_programs(1) - 1, next_ref[...] * gain + shift, 0.0)
    t = i * POOL_TILE + lax.broadcasted_iota(jnp.int32, (POOL_TILE, 1), 0)
    for g, w in enumerate(POOL_WINDOWS):
        cols = slice(g * POOL_GROUP, (g + 1) * POOL_GROUP)
        half = w // 2
        total = e_ref[POOL_HALO - half:POOL_HALO - half + POOL_TILE, cols]
        for d in range(-half + 1, w - half):
            total = total + e_ref[POOL_HALO + d:POOL_HALO + d + POOL_TILE, cols]
        count = jnp.minimum(t + (w - half), SEQ) - jnp.maximum(t - half, 0)
        o_ref[:, cols] = (total / count.astype(F32) - e_ref[mid_rows, cols]).astype(BF16)


def _pool(x, m4, layer):
    batch = x.shape[0]
    halo_per_tile = POOL_TILE // POOL_HALO
    n_halo = SEQ // POOL_HALO
    return pl.pallas_call(
        _pool_kernel,
        grid=(batch, SEQ // POOL_TILE),
        in_specs=[
            pl.BlockSpec((None, POOL_TILE, D_MODEL), lambda b, i: (b, i, 0)),
            pl.BlockSpec((None, POOL_HALO, D_MODEL),
                         lambda b, i: (b, jnp.maximum(i * halo_per_tile - 1, 0), 0)),
            pl.BlockSpec((None, POOL_HALO, D_MODEL),
                         lambda b, i: (b, jnp.minimum((i + 1) * halo_per_tile, n_halo - 1), 0)),
            _mod_spec(layer, lambda b, i: b, 0),
            _mod_spec(layer, lambda b, i: b, 1),
        ],
        out_specs=pl.BlockSpec((None, POOL_TILE, D_MODEL), lambda b, i: (b, i, 0)),
        out_shape=jax.ShapeDtypeStruct((batch, SEQ, D_MODEL), BF16),
        scratch_shapes=[pltpu.VMEM((POOL_TILE + 2 * POOL_HALO, D_MODEL), F32)],
        compiler_params=_params("parallel", "parallel"),
        name="multiscale_pool",
    )(x, x, x, m4, m4)


POST_TILE = 512
FF_CHUNK = D_FF // 2


def _post_kernel(*refs, grouped):
    if grouped:
        (x_ref, mix_ref, wmix_ref, mscale_ref, g1_ref, sh2_ref, sc2_ref, g2_ref,
         lng_ref, lnb_ref, win_ref, wout_ref, o_ref) = refs
    else:
        (x_ref, mix_ref, wmix_ref, g1_ref, sh2_ref, sc2_ref, g2_ref,
         lng_ref, lnb_ref, win_ref, wout_ref, o_ref) = refs
    if grouped:
        y = jnp.concatenate(
            [_dot(mix_ref[:, g * POOL_GROUP:(g + 1) * POOL_GROUP], wmix_ref[g])
             for g in range(len(POOL_WINDOWS))], axis=-1) * mscale_ref[...]
    else:
        y = _dot(mix_ref[...], wmix_ref[...])
    x1 = _layer_norm(ALPHA * x_ref[...] + g1_ref[...] * y, lng_ref[0:1, :], lnb_ref[0:1, :])
    h = (x1 * (1.0 + sc2_ref[...]) + sh2_ref[...]).astype(BF16)
    acc = None
    for c in range(D_FF // FF_CHUNK):
        gate = _dot(h, win_ref[:, c * FF_CHUNK:(c + 1) * FF_CHUNK])
        up = _dot(h, win_ref[:, D_FF + c * FF_CHUNK:D_FF + (c + 1) * FF_CHUNK])
        act = (gate * _sigmoid(gate) * up).astype(BF16)
        part = _dot(act, wout_ref[c * FF_CHUNK:(c + 1) * FF_CHUNK, :])
        acc = part if acc is None else acc + part
    o_ref[...] = _layer_norm(ALPHA * x1 + g2_ref[...] * acc, lng_ref[1:2, :], lnb_ref[1:2, :])


def _post(x, mix, w_mix, mix_scale, m4, layer, ln_g, ln_b, w_in, w_out):
    batch = x.shape[0]
    grouped = mix_scale is not None
    tile = lambda: pl.BlockSpec((None, POST_TILE, D_MODEL), lambda b, i: (b, i, 0))
    mod = lambda which: _mod_spec(layer, lambda b, i: b, which)
    in_specs = [tile(), tile(), _resident(w_mix.shape)]
    args = [x, mix, w_mix]
    if grouped:
        in_specs.append(_resident((1, D_MODEL)))
        args.append(mix_scale.reshape(1, D_MODEL))
    in_specs += [mod(2), mod(3), mod(4), mod(5),
                 _resident((2, D_MODEL)), _resident((2, D_MODEL)),
                 _resident(w_in.shape), _resident(w_out.shape)]
    args += [m4, m4, m4, m4, ln_g, ln_b, w_in, w_out]
    return pl.pallas_call(
        functools.partial(_post_kernel, grouped=grouped),
        grid=(batch, SEQ // POST_TILE),
        in_specs=in_specs,
        out_specs=tile(),
        out_shape=jax.ShapeDtypeStruct((batch, SEQ, D_MODEL), F32),
        compiler_params=_params("parallel", "parallel"),
        name="mix_out_ffn_pool" if grouped else "mix_out_ffn_attn",
    )(*args)


def _rope_tables():
    rows = SEQ // GRID_W
    row = jnp.repeat(jnp.arange(rows, dtype=F32), GRID_W)
    col = jnp.tile(jnp.arange(GRID_W, dtype=F32), rows)
    inv = jnp.power(ROPE_BASE, -jnp.arange(ROPE_PAIRS_PER_AXIS, dtype=F32) / ROPE_PAIRS_PER_AXIS)
    ang = jnp.concatenate([row[:, None] * inv, col[:, None] * inv], axis=-1)
    cos = jnp.cos(ang)
    sin = jnp.sin(ang)
    return jnp.tile(cos, (1, 4)), jnp.concatenate([-sin, -sin, sin, sin], axis=-1)


def _head_permutation():
    half = HEAD_DIM // 2
    base = jnp.concatenate([jnp.arange(0, half), jnp.arange(HEAD_DIM, HEAD_DIM + half),
                            jnp.arange(half, HEAD_DIM), jnp.arange(HEAD_DIM + half, HEAD_WIDTH)])
    return (jnp.arange(N_HEADS)[:, None] * HEAD_WIDTH + base[None, :]).reshape(-1)


def kernel(x, c, ctx, c_ctx, w_mod, b_mod, ln_g, ln_b, attn_w_qkv, attn_lambda, attn_subln_g,
           attn_w_o, pool_w, pool_scale, ffn_w_in, ffn_w_out):
    batch = x.shape[0]
    assert x.shape == (batch, SEQ, D_MODEL) and ctx.shape == (batch, CTX_LEN, D_MODEL)
    assert batch + 1 <= COND_ROWS

    cond = jnp.concatenate([c, c_ctx[None, :], jnp.zeros((COND_ROWS - batch - 1, D_MODEL), F32)], axis=0)
    m4 = _modulation(cond, w_mod, b_mod).reshape(DEPTH, COND_ROWS, 1, N_MOD * D_MODEL)

    perm = _head_permutation()
    w_qkv = attn_w_qkv[0]
    wq = (w_qkv[:, :D_MODEL][:, perm] * (HEAD_DIM ** -0.5)).astype(BF16)
    wk = w_qkv[:, D_MODEL:2 * D_MODEL][:, perm].astype(BF16)
    wv = w_qkv[:, 2 * D_MODEL:].astype(BF16)
    cos_t, sin_t = _rope_tables()
    q, k_all, v_all = _qkv_project(x, ctx, m4, cos_t, sin_t, wq, wk, wv)
    o = _attention(q, k_all, v_all, attn_lambda[0], attn_subln_g[0])
    x = _post(x, o, attn_w_o[0].astype(BF16), None, m4, 0, ln_g[0], ln_b[0],
              ffn_w_in[0].astype(BF16), ffn_w_out[0].astype(BF16))

    pooled = _pool(x, m4, 1)
    x = _post(x, pooled, pool_w[0].astype(BF16), pool_scale[0], m4, 1, ln_g[1], ln_b[1],
              ffn_w_in[1].astype(BF16), ffn_w_out[1].astype(BF16))
    return x
```

```python
import functools
import math

import jax
import jax.numpy as jnp
from jax import lax
from jax.experimental import pallas as pl
from jax.experimental.pallas import tpu as pltpu

D_MODEL = 1024
SEQ = 2048
DEPTH = 2
GRID_W = 64
CTX_LEN = 256
N_HEADS = 8
HEAD_DIM = 64
HEAD_WIDTH = 2 * HEAD_DIM
ROPE_BASE = 10000.0
ROPE_PAIRS_PER_AXIS = HEAD_DIM // 4
POOL_WINDOWS = (2, 4, 8, 16)
POOL_GROUP = D_MODEL // len(POOL_WINDOWS)
POOL_HALO = 8
D_FF = 2816
ALPHA = (2.0 * DEPTH) ** 0.25
LN_EPS = 1e-5
N_MOD = 6
LAM_INIT_0 = 0.8 - 0.6 * math.exp(-0.3 * 0)

COND_ROWS = 24
KV_LEN = CTX_LEN + SEQ

BF16 = jnp.bfloat16
F32 = jnp.float32

VMEM_LIMIT_BYTES = 56 * 1024 * 1024


def _dot(a, b):
    return jnp.dot(a, b, preferred_element_type=F32)


def _dot_nt(a, b):
    return lax.dot_general(a, b, (((1,), (1,)), ((), ())), preferred_element_type=F32)


def _sigmoid(v):
    return 1.0 / (1.0 + jnp.exp(-v))


def _layer_norm(z, g, b):
    mu = jnp.mean(z, axis=-1, keepdims=True)
    d = z - mu
    var = jnp.mean(d * d, axis=-1, keepdims=True)
    return d * lax.rsqrt(var + LN_EPS) * g + b


def _params(*semantics):
    return pltpu.CompilerParams(dimension_semantics=semantics, vmem_limit_bytes=VMEM_LIMIT_BYTES)


def _resident(shape):
    return pl.BlockSpec(shape, lambda *_: (0,) * len(shape), pipeline_mode=pl.Buffered(1))


MOD_TILE_N = 1024


def _mod_kernel(c_ref, w_ref, b_ref, o_ref):
    c = c_ref[...]
    s = (c * _sigmoid(c)).astype(BF16)
    o_ref[...] = _dot(s, w_ref[...].astype(BF16)) + b_ref[...]


def _modulation(cond, w_mod, b_mod):
    n = N_MOD * D_MODEL
    return pl.pallas_call(
        _mod_kernel,
        grid=(DEPTH, n // MOD_TILE_N),
        in_specs=[
            pl.BlockSpec((COND_ROWS, D_MODEL), lambda l, j: (0, 0)),
            pl.BlockSpec((None, D_MODEL, MOD_TILE_N), lambda l, j: (l, 0, j)),
            pl.BlockSpec((None, 1, MOD_TILE_N), lambda l, j: (l, 0, j)),
        ],
        out_specs=pl.BlockSpec((None, COND_ROWS, MOD_TILE_N), lambda l, j: (l, 0, j)),
        out_shape=jax.ShapeDtypeStruct((DEPTH, COND_ROWS, n), F32),
        compiler_params=_params("parallel", "parallel"),
        name="adaln_modulation",
    )(cond, w_mod, b_mod.reshape(DEPTH, 1, n))


def _mod_spec(layer, row_of, which):
    return pl.BlockSpec((None, None, 1, D_MODEL), lambda *ids: (layer, row_of(*ids), 0, which))


QKV_TILE = 256


def _qkv_kernel(x_ref, ctx_ref, sh_ref, sc_ref, csh_ref, csc_ref, cos_ref, sin_ref,
                wq_ref, wk_ref, wv_ref, q_ref, k_ref, v_ref):
    is_ctx = pl.program_id(1) == 0
    src = jnp.where(is_ctx, ctx_ref[...], x_ref[...])
    shift = jnp.where(is_ctx, csh_ref[...], sh_ref[...])
    scale = jnp.where(is_ctx, csc_ref[...], sc_ref[...])
    h = (src * (1.0 + scale) + shift).astype(BF16)
    cos = cos_ref[...]
    sin = sin_ref[...]
    q = _dot(h, wq_ref[...])
    k = _dot(h, wk_ref[...])
    v_ref[...] = _dot(h, wv_ref[...]).T.astype(BF16)
    for hd in range(N_HEADS):
        sl = slice(hd * HEAD_WIDTH, (hd + 1) * HEAD_WIDTH)
        qh = q[:, sl]
        kh = k[:, sl]
        q_ref[:, sl] = (qh * cos + pltpu.roll(qh, HEAD_DIM, 1) * sin).astype(BF16)
        kr = kh * cos + pltpu.roll(kh, HEAD_DIM, 1) * sin
        k_ref[:, sl] = jnp.where(is_ctx, kh, kr).astype(BF16)


def _qkv_project(x, ctx, m4, cos_t, sin_t, wq, wk, wv):
    batch = x.shape[0]
    n_tiles = KV_LEN // QKV_TILE
    ctx_tiles = CTX_LEN // QKV_TILE
    assert ctx_tiles == 1

    def lat(i):
        return jnp.maximum(i - ctx_tiles, 0)

    tile = lambda f: pl.BlockSpec((None, QKV_TILE, D_MODEL), f)
    return pl.pallas_call(
        _qkv_kernel,
        grid=(batch, n_tiles),
        in_specs=[
            tile(lambda b, i: (b, lat(i), 0)),
            tile(lambda b, i: (b, 0, 0)),
            _mod_spec(0, lambda b, i: b, 0),
            _mod_spec(0, lambda b, i: b, 1),
            _mod_spec(0, lambda b, i: batch, 0),
            _mod_spec(0, lambda b, i: batch, 1),
            pl.BlockSpec((QKV_TILE, HEAD_WIDTH), lambda b, i: (lat(i), 0)),
            pl.BlockSpec((QKV_TILE, HEAD_WIDTH), lambda b, i: (lat(i), 0)),
            _resident((D_MODEL, D_MODEL)),
            _resident((D_MODEL, D_MODEL)),
            _resident((D_MODEL, D_MODEL)),
        ],
        out_specs=[
            tile(lambda b, i: (b, lat(i), 0)),
            tile(lambda b, i: (b, i, 0)),
            pl.BlockSpec((None, D_MODEL, QKV_TILE), lambda b, i: (b, 0, i)),
        ],
        out_shape=[
            jax.ShapeDtypeStruct((batch, SEQ, D_MODEL), BF16),
            jax.ShapeDtypeStruct((batch, KV_LEN, D_MODEL), BF16),
            jax.ShapeDtypeStruct((batch, D_MODEL, KV_LEN), BF16),
        ],
        compiler_params=_params("parallel", "arbitrary"),
        name="qkv_rope",
    )(x, ctx, m4, m4, m4, m4, cos_t, sin_t, wq, wk, wv)


ATTN_Q_TILE = 256
ATTN_FOLD_ROWS = 64
ATTN_SUM_ROWS = 16
ATTN_SLOTS = 3


def _attn_kernel(lam_ref, g_ref, q_ref, k_ref, vt_ref, o_ref, *scratch_refs):
    lp = lam_ref[...]
    lam = (jnp.exp(jnp.sum(lp[0:1] * lp[1:2], axis=-1, keepdims=True))
           - jnp.exp(jnp.sum(lp[2:3] * lp[3:4], axis=-1, keepdims=True)) + LAM_INIT_0)
    out_gain = g_ref[...] * (1.0 - LAM_INIT_0)
    lane = lax.broadcasted_iota(jnp.int32, (1, HEAD_WIDTH), 1)
    first = jnp.where((lane // (HEAD_DIM // 2)) % 2 == 0, 1.0, 0.0)
    k = k_ref[...]
    ones_row = lax.broadcasted_iota(jnp.int32, (ATTN_SUM_ROWS, KV_LEN), 0) == 0
    vt = jnp.concatenate([vt_ref[...], jnp.where(ones_row, 1.0, 0.0).astype(BF16)], axis=0)

    def tile_rows(t):
        return slice(t * ATTN_Q_TILE, (t + 1) * ATTN_Q_TILE)

    def reduce_keys(op, a):
        slabs = a.reshape(KV_LEN // ATTN_FOLD_ROWS, ATTN_FOLD_ROWS, ATTN_Q_TILE)
        return op(op(slabs, axis=0), axis=0, keepdims=True)

    masks = (first, 1.0 - first)

    def slot(kind, t, c):
        return scratch_refs[(kind * ATTN_SLOTS + t % ATTN_SLOTS) * 2 + c]

    def scores(t, c):
        q = q_ref[tile_rows(t), :].astype(F32)
        s = _dot_nt(k, (q * masks[c]).astype(BF16))
        slot(0, t, c)[...] = s
        slot(1, t, c)[...] = reduce_keys(jnp.max, s)

    def unnormalised(t, c):
        slot(2, t, c)[...] = jnp.exp2(slot(0, t, c)[...] - slot(1, t, c)[...]).astype(BF16)
        ol = _dot(vt, slot(2, t, c)[...])
        return ol[:HEAD_WIDTH], ol[HEAD_WIDTH:HEAD_WIDTH + 1]

    def combine(t, o1, l1, o2, l2):
        o = o1 * (1.0 / l1) - o2 * (lam / l2)
        o = o * lax.rsqrt(jnp.mean(o * o, axis=0, keepdims=True) + LN_EPS) * out_gain
        o_ref[tile_rows(t), :] = o.T.astype(BF16)

    n_tiles = SEQ // ATTN_Q_TILE
    ahead = ATTN_SLOTS - 1
    for t in range(min(ahead, n_tiles)):
        scores(t, 0)
        scores(t, 1)
    for t in range(n_tiles):
        parts = []
        for c in range(2):
            if t + ahead < n_tiles:
                scores(t + ahead, c)
            parts += unnormalised(t, c)
        combine(t, *parts)


def _attention(q, k_all, vt_all, lam_params, subln_g):
    batch = q.shape[0]
    head_cols = lambda rows: pl.BlockSpec((None, rows, HEAD_WIDTH), lambda b, h: (b, 0, h))
    return pl.pallas_call(
        _attn_kernel,
        grid=(batch, N_HEADS),
        in_specs=[
            pl.BlockSpec((4, HEAD_DIM), lambda b, h: (0, 0)),
            pl.BlockSpec((HEAD_WIDTH, 1), lambda b, h: (0, 0)),
            head_cols(SEQ),
            head_cols(KV_LEN),
            pl.BlockSpec((None, HEAD_WIDTH, KV_LEN), lambda b, h: (b, h, 0)),
        ],
        out_specs=head_cols(SEQ),
        out_shape=jax.ShapeDtypeStruct((batch, SEQ, D_MODEL), BF16),
        scratch_shapes=([pltpu.VMEM((KV_LEN, ATTN_Q_TILE), F32) for _ in range(2 * ATTN_SLOTS)]
                        + [pltpu.VMEM((1, ATTN_Q_TILE), F32) for _ in range(2 * ATTN_SLOTS)]
                        + [pltpu.VMEM((KV_LEN, ATTN_Q_TILE), BF16) for _ in range(2 * ATTN_SLOTS)]),
        compiler_params=_params("parallel", "parallel"),
        name="diff_attention",
    )(lam_params, subln_g.reshape(HEAD_WIDTH, 1), q, k_all, vt_all)


POOL_TILE = 512


def _pool_kernel(x_ref, prev_ref, next_ref, sh_ref, sc_ref, o_ref, e_ref):
    i = pl.program_id(1)
    gain = 1.0 + sc_ref[...]
    shift = sh_ref[...]
    lo_rows = slice(0, POOL_HALO)
    mid_rows = slice(POOL_HALO, POOL_HALO + POOL_TILE)
    hi_rows = slice(POOL_HALO + POOL_TILE, POOL_HALO + POOL_TILE + POOL_HALO)
    e_ref[lo_rows, :] = jnp.where(i > 0, prev_ref[...] * gain + shift, 0.0)
    e_ref[mid_rows, :] = x_ref[...] * gain + shift
    e_ref[hi_rows, :] = jnp.where(i < pl.num_programs(1) - 1, next_ref[...] * gain + shift, 0.0)
    t = i * POOL_TILE + lax.broadcasted_iota(jnp.int32, (POOL_TILE, 1), 0)
    for g, w in enumerate(POOL_WINDOWS):
        cols = slice(g * POOL_GROUP, (g + 1) * POOL_GROUP)
        half = w // 2
        total = e_ref[POOL_HALO - half:POOL_HALO - half + POOL_TILE, cols]
        for d in range(-half + 1, w - half):
            total = total + e_ref[POOL_HALO + d:POOL_HALO + d + POOL_TILE, cols]
        count = jnp.minimum(t + (w - half), SEQ) - jnp.maximum(t - half, 0)
        o_ref[:, cols] = (total / count.astype(F32) - e_ref[mid_rows, cols]).astype(BF16)


def _pool(x, m4, layer):
    batch = x.shape[0]
    halo_per_tile = POOL_TILE // POOL_HALO
    n_halo = SEQ // POOL_HALO
    return pl.pallas_call(
        _pool_kernel,
        grid=(batch, SEQ // POOL_TILE),
        in_specs=[
            pl.BlockSpec((None, POOL_TILE, D_MODEL), lambda b, i: (b, i, 0)),
            pl.BlockSpec((None, POOL_HALO, D_MODEL),
                         lambda b, i: (b, jnp.maximum(i * halo_per_tile - 1, 0), 0)),
            pl.BlockSpec((None, POOL_HALO, D_MODEL),
                         lambda b, i: (b, jnp.minimum((i + 1) * halo_per_tile, n_halo - 1), 0)),
            _mod_spec(layer, lambda b, i: b, 0),
            _mod_spec(layer, lambda b, i: b, 1),
        ],
        out_specs=pl.BlockSpec((None, POOL_TILE, D_MODEL), lambda b, i: (b, i, 0)),
        out_shape=jax.ShapeDtypeStruct((batch, SEQ, D_MODEL), BF16),
        scratch_shapes=[pltpu.VMEM((POOL_TILE + 2 * POOL_HALO, D_MODEL), F32)],
        compiler_params=_params("parallel", "parallel"),
        name="multiscale_pool",
    )(x, x, x, m4, m4)


POST_TILE = 512
FF_CHUNK = D_FF // 2


def _post_kernel(*refs, grouped):
    if grouped:
        (x_ref, mix_ref, wmix_ref, mscale_ref, g1_ref, sh2_ref, sc2_ref, g2_ref,
         lng_ref, lnb_ref, win_ref, wout_ref, o_ref) = refs
    else:
        (x_ref, mix_ref, wmix_ref, g1_ref, sh2_ref, sc2_ref, g2_ref,
         lng_ref, lnb_ref, win_ref, wout_ref, o_ref) = refs
    if grouped:
        y = jnp.concatenate(
            [_dot(mix_ref[:, g * POOL_GROUP:(g + 1) * POOL_GROUP], wmix_ref[g])
             for g in range(len(POOL_WINDOWS))], axis=-1) * mscale_ref[...]
    else:
        y = _dot(mix_ref[...], wmix_ref[...])
    x1 = _layer_norm(ALPHA * x_ref[...] + g1_ref[...] * y, lng_ref[0:1, :], lnb_ref[0:1, :])
    h = (x1 * (1.0 + sc2_ref[...]) + sh2_ref[...]).astype(BF16)
    acc = None
    for c in range(D_FF // FF_CHUNK):
        gate = _dot(h, win_ref[:, c * FF_CHUNK:(c + 1) * FF_CHUNK])
        up = _dot(h, win_ref[:, D_FF + c * FF_CHUNK:D_FF + (c + 1) * FF_CHUNK])
        act = (gate * _sigmoid(gate) * up).astype(BF16)
        part = _dot(act, wout_ref[c * FF_CHUNK:(c + 1) * FF_CHUNK, :])
        acc = part if acc is None else acc + part
    o_ref[...] = _layer_norm(ALPHA * x1 + g2_ref[...] * acc, lng_ref[1:2, :], lnb_ref[1:2, :])


def _post(x, mix, w_mix, mix_scale, m4, layer, ln_g, ln_b, w_in, w_out):
    batch = x.shape[0]
    grouped = mix_scale is not None
    tile = lambda: pl.BlockSpec((None, POST_TILE, D_MODEL), lambda b, i: (b, i, 0))
    mod = lambda which: _mod_spec(layer, lambda b, i: b, which)
    in_specs = [tile(), tile(), _resident(w_mix.shape)]
    args = [x, mix, w_mix]
    if grouped:
        in_specs.append(_resident((1, D_MODEL)))
        args.append(mix_scale.reshape(1, D_MODEL))
    in_specs += [mod(2), mod(3), mod(4), mod(5),
                 _resident((2, D_MODEL)), _resident((2, D_MODEL)),
                 _resident(w_in.shape), _resident(w_out.shape)]
    args += [m4, m4, m4, m4, ln_g, ln_b, w_in, w_out]
    return pl.pallas_call(
        functools.partial(_post_kernel, grouped=grouped),
        grid=(batch, SEQ // POST_TILE),
        in_specs=in_specs,
        out_specs=tile(),
        out_shape=jax.ShapeDtypeStruct((batch, SEQ, D_MODEL), F32),
        compiler_params=_params("parallel", "parallel"),
        name="mix_out_ffn_pool" if grouped else "mix_out_ffn_attn",
    )(*args)


def _rope_tables():
    rows = SEQ // GRID_W
    row = jnp.repeat(jnp.arange(rows, dtype=F32), GRID_W)
    col = jnp.tile(jnp.arange(GRID_W, dtype=F32), rows)
    inv = jnp.power(ROPE_BASE, -jnp.arange(ROPE_PAIRS_PER_AXIS, dtype=F32) / ROPE_PAIRS_PER_AXIS)
    ang = jnp.concatenate([row[:, None] * inv, col[:, None] * inv], axis=-1)
    cos = jnp.cos(ang)
    sin = jnp.sin(ang)
    return jnp.tile(cos, (1, 4)), jnp.concatenate([-sin, -sin, sin, sin], axis=-1)


def _head_permutation():
    half = HEAD_DIM // 2
    base = jnp.concatenate([jnp.arange(0, half), jnp.arange(HEAD_DIM, HEAD_DIM + half),
                            jnp.arange(half, HEAD_DIM), jnp.arange(HEAD_DIM + half, HEAD_WIDTH)])
    return (jnp.arange(N_HEADS)[:, None] * HEAD_WIDTH + base[None, :]).reshape(-1)


def kernel(x, c, ctx, c_ctx, w_mod, b_mod, ln_g, ln_b, attn_w_qkv, attn_lambda, attn_subln_g,
           attn_w_o, pool_w, pool_scale, ffn_w_in, ffn_w_out):
    batch = x.shape[0]
    assert x.shape == (batch, SEQ, D_MODEL) and ctx.shape == (batch, CTX_LEN, D_MODEL)
    assert batch + 1 <= COND_ROWS

    cond = jnp.concatenate([c, c_ctx[None, :], jnp.zeros((COND_ROWS - batch - 1, D_MODEL), F32)], axis=0)
    m4 = _modulation(cond, w_mod, b_mod).reshape(DEPTH, COND_ROWS, 1, N_MOD * D_MODEL)

    perm = _head_permutation()
    w_qkv = attn_w_qkv[0]
    wq = (w_qkv[:, :D_MODEL][:, perm] * (HEAD_DIM ** -0.5 * math.log2(math.e))).astype(BF16)
    wk = w_qkv[:, D_MODEL:2 * D_MODEL][:, perm].astype(BF16)
    wv = w_qkv[:, 2 * D_MODEL:].astype(BF16)
    cos_t, sin_t = _rope_tables()
    q, k_all, vt_all = _qkv_project(x, ctx, m4, cos_t, sin_t, wq, wk, wv)
    o = _attention(q, k_all, vt_all, attn_lambda[0], attn_subln_g[0])
    x = _post(x, o, attn_w_o[0].astype(BF16), None, m4, 0, ln_g[0], ln_b[0],
              ffn_w_in[0].astype(BF16), ffn_w_out[0].astype(BF16))

    pooled = _pool(x, m4, 1)
    x = _post(x, pooled, pool_w[0].astype(BF16), pool_scale[0], m4, 1, ln_g[1], ln_b[1],
              ffn_w_in[1].astype(BF16), ffn_w_out[1].astype(BF16))
    return x
```

```python
import functools
import math

import jax
import jax.numpy as jnp
from jax import lax
from jax.experimental import pallas as pl
from jax.experimental.pallas import tpu as pltpu

D_MODEL = 1024
SEQ = 2048
DEPTH = 2
GRID_W = 64
CTX_LEN = 256
N_HEADS = 8
HEAD_DIM = 64
HEAD_WIDTH = 2 * HEAD_DIM
ROPE_BASE = 10000.0
ROPE_PAIRS_PER_AXIS = HEAD_DIM // 4
POOL_WINDOWS = (2, 4, 8, 16)
POOL_GROUP = D_MODEL // len(POOL_WINDOWS)
POOL_HALO = 8
D_FF = 2816
ALPHA = (2.0 * DEPTH) ** 0.25
LN_EPS = 1e-5
N_MOD = 6
LAM_INIT_0 = 0.8 - 0.6 * math.exp(-0.3 * 0)

COND_ROWS = 24
KV_LEN = CTX_LEN + SEQ

BF16 = jnp.bfloat16
F32 = jnp.float32

VMEM_LIMIT_BYTES = 56 * 1024 * 1024


def _dot(a, b):
    return jnp.dot(a, b, preferred_element_type=F32)


def _dot_nt(a, b):
    return lax.dot_general(a, b, (((1,), (1,)), ((), ())), preferred_element_type=F32)


def _sigmoid(v):
    return 1.0 / (1.0 + jnp.exp(-v))


def _layer_norm(z, g, b):
    mu = jnp.mean(z, axis=-1, keepdims=True)
    d = z - mu
    var = jnp.mean(d * d, axis=-1, keepdims=True)
    return d * lax.rsqrt(var + LN_EPS) * g + b


def _params(*semantics):
    return pltpu.CompilerParams(dimension_semantics=semantics, vmem_limit_bytes=VMEM_LIMIT_BYTES)


def _resident(shape):
    return pl.BlockSpec(shape, lambda *_: (0,) * len(shape), pipeline_mode=pl.Buffered(1))


MOD_TILE_N = 1024


def _mod_kernel(c_ref, w_ref, b_ref, o_ref):
    c = c_ref[...]
    s = (c * _sigmoid(c)).astype(BF16)
    o_ref[...] = _dot(s, w_ref[...].astype(BF16)) + b_ref[...]


def _modulation(cond, w_mod, b_mod):
    n = N_MOD * D_MODEL
    return pl.pallas_call(
        _mod_kernel,
        grid=(DEPTH, n // MOD_TILE_N),
        in_specs=[
            pl.BlockSpec((COND_ROWS, D_MODEL), lambda l, j: (0, 0)),
            pl.BlockSpec((None, D_MODEL, MOD_TILE_N), lambda l, j: (l, 0, j)),
            pl.BlockSpec((None, 1, MOD_TILE_N), lambda l, j: (l, 0, j)),
        ],
        out_specs=pl.BlockSpec((None, COND_ROWS, MOD_TILE_N), lambda l, j: (l, 0, j)),
        out_shape=jax.ShapeDtypeStruct((DEPTH, COND_ROWS, n), F32),
        compiler_params=_params("parallel", "parallel"),
        name="adaln_modulation",
    )(cond, w_mod, b_mod.reshape(DEPTH, 1, n))


def _mod_spec(layer, row_of, which):
    return pl.BlockSpec((None, None, 1, D_MODEL), lambda *ids: (layer, row_of(*ids), 0, which))


QKV_TILE = 256


def _qkv_kernel(x_ref, ctx_ref, sh_ref, sc_ref, csh_ref, csc_ref, cos_ref, sin_ref,
                wq_ref, wk_ref, wv_ref, q_ref, k_ref, v_ref):
    is_ctx = pl.program_id(1) == 0
    src = jnp.where(is_ctx, ctx_ref[...], x_ref[...])
    shift = jnp.where(is_ctx, csh_ref[...], sh_ref[...])
    scale = jnp.where(is_ctx, csc_ref[...], sc_ref[...])
    h = (src * (1.0 + scale) + shift).astype(BF16)
    cos = cos_ref[...]
    sin = sin_ref[...]
    q = _dot(h, wq_ref[...])
    k = _dot(h, wk_ref[...])
    v_ref[...] = _dot(h, wv_ref[...]).T.astype(BF16)
    for hd in range(N_HEADS):
        sl = slice(hd * HEAD_WIDTH, (hd + 1) * HEAD_WIDTH)
        qh = q[:, sl]
        kh = k[:, sl]
        q_ref[:, sl] = (qh * cos + pltpu.roll(qh, HEAD_DIM, 1) * sin).astype(BF16)
        kr = kh * cos + pltpu.roll(kh, HEAD_DIM, 1) * sin
        k_ref[:, sl] = jnp.where(is_ctx, kh, kr).astype(BF16)


def _qkv_project(x, ctx, m4, cos_t, sin_t, wq, wk, wv):
    batch = x.shape[0]
    n_tiles = KV_LEN // QKV_TILE
    ctx_tiles = CTX_LEN // QKV_TILE
    assert ctx_tiles == 1

    def lat(i):
        return jnp.maximum(i - ctx_tiles, 0)

    tile = lambda f: pl.BlockSpec((None, QKV_TILE, D_MODEL), f)
    return pl.pallas_call(
        _qkv_kernel,
        grid=(batch, n_tiles),
        in_specs=[
            tile(lambda b, i: (b, lat(i), 0)),
            tile(lambda b, i: (b, 0, 0)),
            _mod_spec(0, lambda b, i: b, 0),
            _mod_spec(0, lambda b, i: b, 1),
            _mod_spec(0, lambda b, i: batch, 0),
            _mod_spec(0, lambda b, i: batch, 1),
            pl.BlockSpec((QKV_TILE, HEAD_WIDTH), lambda b, i: (lat(i), 0)),
            pl.BlockSpec((QKV_TILE, HEAD_WIDTH), lambda b, i: (lat(i), 0)),
            _resident((D_MODEL, D_MODEL)),
            _resident((D_MODEL, D_MODEL)),
            _resident((D_MODEL, D_MODEL)),
        ],
        out_specs=[
            tile(lambda b, i: (b, lat(i), 0)),
            tile(lambda b, i: (b, i, 0)),
            pl.BlockSpec((None, D_MODEL, QKV_TILE), lambda b, i: (b, 0, i)),
        ],
        out_shape=[
            jax.ShapeDtypeStruct((batch, SEQ, D_MODEL), BF16),
            jax.ShapeDtypeStruct((batch, KV_LEN, D_MODEL), BF16),
            jax.ShapeDtypeStruct((batch, D_MODEL, KV_LEN), BF16),
        ],
        compiler_params=_params("parallel", "arbitrary"),
        name="qkv_rope",
    )(x, ctx, m4, m4, m4, m4, cos_t, sin_t, wq, wk, wv)


ATTN_Q_TILE = 256
ATTN_FOLD_ROWS = 64
ATTN_SUM_ROWS = 16
ATTN_SLOTS = 3


def _attn_kernel(lam_ref, g_ref, q_ref, k_ref, vt_ref, o_ref, *scratch_refs):
    lp = lam_ref[...]
    lam = (jnp.exp(jnp.sum(lp[0:1] * lp[1:2], axis=-1, keepdims=True))
           - jnp.exp(jnp.sum(lp[2:3] * lp[3:4], axis=-1, keepdims=True)) + LAM_INIT_0)
    out_gain = g_ref[...] * (1.0 - LAM_INIT_0)
    lane = lax.broadcasted_iota(jnp.int32, (1, HEAD_WIDTH), 1)
    first = jnp.where((lane // (HEAD_DIM // 2)) % 2 == 0, 1.0, 0.0)
    k = k_ref[...]
    ones_row = lax.broadcasted_iota(jnp.int32, (ATTN_SUM_ROWS, KV_LEN), 0) == 0
    vt = jnp.concatenate([vt_ref[...], jnp.where(ones_row, 1.0, 0.0).astype(BF16)], axis=0)

    def tile_rows(t):
        return slice(t * ATTN_Q_TILE, (t + 1) * ATTN_Q_TILE)

    def reduce_keys(op, a):
        slabs = a.reshape(KV_LEN // ATTN_FOLD_ROWS, ATTN_FOLD_ROWS, ATTN_Q_TILE)
        return op(op(slabs, axis=0), axis=0, keepdims=True)

    masks = (first, 1.0 - first)

    def slot(kind, t, c):
        return scratch_refs[(kind * ATTN_SLOTS + t % ATTN_SLOTS) * 2 + c]

    def scores(t, c):
        q = q_ref[tile_rows(t), :].astype(F32)
        s = _dot_nt(k, (q * masks[c]).astype(BF16))
        slot(0, t, c)[...] = s
        slot(1, t, c)[...] = reduce_keys(jnp.max, s)

    def unnormalised(t, c):
        slot(2, t, c)[...] = jnp.exp2(slot(0, t, c)[...] - slot(1, t, c)[...]).astype(BF16)
        ol = _dot(vt, slot(2, t, c)[...])
        return ol[:HEAD_WIDTH], ol[HEAD_WIDTH:HEAD_WIDTH + 1]

    def combine(t, o1, l1, o2, l2):
        o = o1 * (1.0 / l1) - o2 * (lam / l2)
        o = o * lax.rsqrt(jnp.mean(o * o, axis=0, keepdims=True) + LN_EPS) * out_gain
        o_ref[tile_rows(t), :] = o.T.astype(BF16)

    n_tiles = SEQ // ATTN_Q_TILE
    ahead = ATTN_SLOTS - 1
    for t in range(min(ahead, n_tiles)):
        scores(t, 0)
        scores(t, 1)
    for t in range(n_tiles):
        parts = []
        for c in range(2):
            if t + ahead < n_tiles:
                scores(t + ahead, c)
            parts += unnormalised(t, c)
        combine(t, *parts)


def _attention(q, k_all, vt_all, lam_params, subln_g):
    batch = q.shape[0]
    head_cols = lambda rows: pl.BlockSpec((None, rows, HEAD_WIDTH), lambda b, h: (b, 0, h))
    return pl.pallas_call(
        _attn_kernel,
        grid=(batch, N_HEADS),
        in_specs=[
            pl.BlockSpec((4, HEAD_DIM), lambda b, h: (0, 0)),
            pl.BlockSpec((HEAD_WIDTH, 1), lambda b, h: (0, 0)),
            head_cols(SEQ),
            head_cols(KV_LEN),
            pl.BlockSpec((None, HEAD_WIDTH, KV_LEN), lambda b, h: (b, h, 0)),
        ],
        out_specs=head_cols(SEQ),
        out_shape=jax.ShapeDtypeStruct((batch, SEQ, D_MODEL), BF16),
        scratch_shapes=([pltpu.VMEM((KV_LEN, ATTN_Q_TILE), F32) for _ in range(2 * ATTN_SLOTS)]
                        + [pltpu.VMEM((1, ATTN_Q_TILE), F32) for _ in range(2 * ATTN_SLOTS)]
                        + [pltpu.VMEM((KV_LEN, ATTN_Q_TILE), BF16) for _ in range(2 * ATTN_SLOTS)]),
        compiler_params=_params("parallel", "parallel"),
        name="diff_attention",
    )(lam_params, subln_g.reshape(HEAD_WIDTH, 1), q, k_all, vt_all)


POOL_TILE = 512


def _pool_kernel(x_ref, prev_ref, next_ref, sh_ref, sc_ref, o_ref, e_ref):
    i = pl.program_id(1)
    gain = 1.0 + sc_ref[...]
    shift = sh_ref[...]
    lo_rows = slice(0, POOL_HALO)
    mid_rows = slice(POOL_HALO, POOL_HALO + POOL_TILE)
    hi_rows = slice(POOL_HALO + POOL_TILE, POOL_HALO + POOL_TILE + POOL_HALO)
    e_ref[lo_rows, :] = jnp.where(i > 0, prev_ref[...] * gain + shift, 0.0)
    e_ref[mid_rows, :] = x_ref[...] * gain + shift
    e_ref[hi_rows, :] = jnp.where(i < pl.num_programs(1) - 1, next_ref[...] * gain + shift, 0.0)
    t = i * POOL_TILE + lax.broadcasted_iota(jnp.int32, (POOL_TILE, 1), 0)
    for g, w in enumerate(POOL_WINDOWS):
        cols = slice(g * POOL_GROUP, (g + 1) * POOL_GROUP)
        half = w // 2
        total = e_ref[POOL_HALO - half:POOL_HALO - half + POOL_TILE, cols]
        for d in range(-half + 1, w - half):
            total = total + e_ref[POOL_HALO + d:POOL_HALO + d + POOL_TILE, cols]
        count = jnp.minimum(t + (w - half), SEQ) - jnp.maximum(t - half, 0)
        o_ref[:, cols] = (total / count.astype(F32) - e_ref[mid_rows, cols]).astype(BF16)


def _pool(x, m4, layer):
    batch = x.shape[0]
    halo_per_tile = POOL_TILE // POOL_HALO
    n_halo = SEQ // POOL_HALO
    return pl.pallas_call(
        _pool_kernel,
        grid=(batch, SEQ // POOL_TILE),
        in_specs=[
            pl.BlockSpec((None, POOL_TILE, D_MODEL), lambda b, i: (b, i, 0)),
            pl.BlockSpec((None, POOL_HALO, D_MODEL),
                         lambda b, i: (b, jnp.maximum(i * halo_per_tile - 1, 0), 0)),
            pl.BlockSpec((None, POOL_HALO, D_MODEL),
                         lambda b, i: (b, jnp.minimum((i + 1) * halo_per_tile, n_halo - 1), 0)),
            _mod_spec(layer, lambda b, i: b, 0),
            _mod_spec(layer, lambda b, i: b, 1),
        ],
        out_specs=pl.BlockSpec((None, POOL_TILE, D_MODEL), lambda b, i: (b, i, 0)),
        out_shape=jax.ShapeDtypeStruct((batch, SEQ, D_MODEL), BF16),
        scratch_shapes=[pltpu.VMEM((POOL_TILE + 2 * POOL_HALO, D_MODEL), F32)],
        compiler_params=_params("parallel", "parallel"),
        name="multiscale_pool",
    )(x, x, x, m4, m4)


POST_TILE = 512
POST_SUBTILES = 2
MXU_TILE = 256
FF_CHUNKS = ((0, 6 * MXU_TILE), (6 * MXU_TILE, D_FF))
assert all(lo % MXU_TILE == 0 and hi % MXU_TILE == 0 for lo, hi in FF_CHUNKS)


def _post_kernel(*refs, grouped):
    if grouped:
        (x_ref, mix_ref, wmix_ref, mscale_ref, g1_ref, sh2_ref, sc2_ref, g2_ref,
         lng_ref, lnb_ref, win_ref, wout_ref, o_ref) = refs
    else:
        (x_ref, mix_ref, wmix_ref, g1_ref, sh2_ref, sc2_ref, g2_ref,
         lng_ref, lnb_ref, win_ref, wout_ref, o_ref) = refs
    sub = POST_TILE // POST_SUBTILES

    def row_group(r):
        rows = slice(r * sub, (r + 1) * sub)
        if grouped:
            y = jnp.concatenate(
                [_dot(mix_ref[rows, g * POOL_GROUP:(g + 1) * POOL_GROUP], wmix_ref[g])
                 for g in range(len(POOL_WINDOWS))], axis=-1) * mscale_ref[...]
        else:
            y = _dot(mix_ref[rows, :], wmix_ref[...])
        x1 = _layer_norm(ALPHA * x_ref[rows, :] + g1_ref[...] * y, lng_ref[0:1, :], lnb_ref[0:1, :])
        h = (x1 * (1.0 + sc2_ref[...]) + sh2_ref[...]).astype(BF16)
        yield
        acc = None
        for lo, hi in FF_CHUNKS:
            gate = _dot(h, win_ref[:, lo:hi])
            up = _dot(h, win_ref[:, D_FF + lo:D_FF + hi])
            act = (gate * _sigmoid(gate) * up).astype(BF16)
            part = _dot(act, wout_ref[lo:hi, :])
            acc = part if acc is None else acc + part
            yield
        o_ref[rows, :] = _layer_norm(ALPHA * x1 + g2_ref[...] * acc, lng_ref[1:2, :], lnb_ref[1:2, :])
        yield

    groups = [row_group(r) for r in range(POST_SUBTILES)]
    n_stages = 2 + len(FF_CHUNKS)
    for step in range(n_stages + POST_SUBTILES - 1):
        for r, group in enumerate(groups):
            if 0 <= step - r < n_stages:
                next(group)


def _post(x, mix, w_mix, mix_scale, m4, layer, ln_g, ln_b, w_in, w_out):
    batch = x.shape[0]
    grouped = mix_scale is not None
    tile = lambda: pl.BlockSpec((None, POST_TILE, D_MODEL), lambda b, i: (b, i, 0))
    mod = lambda which: _mod_spec(layer, lambda b, i: b, which)
    in_specs = [tile(), tile(), _resident(w_mix.shape)]
    args = [x, mix, w_mix]
    if grouped:
        in_specs.append(_resident((1, D_MODEL)))
        args.append(mix_scale.reshape(1, D_MODEL))
    in_specs += [mod(2), mod(3), mod(4), mod(5),
                 _resident((2, D_MODEL)), _resident((2, D_MODEL)),
                 _resident(w_in.shape), _resident(w_out.shape)]
    args += [m4, m4, m4, m4, ln_g, ln_b, w_in, w_out]
    return pl.pallas_call(
        functools.partial(_post_kernel, grouped=grouped),
        grid=(batch, SEQ // POST_TILE),
        in_specs=in_specs,
        out_specs=tile(),
        out_shape=jax.ShapeDtypeStruct((batch, SEQ, D_MODEL), F32),
        compiler_params=_params("parallel", "parallel"),
        name="mix_out_ffn_pool" if grouped else "mix_out_ffn_attn",
    )(*args)


def _rope_tables():
    rows = SEQ // GRID_W
    row = jnp.repeat(jnp.arange(rows, dtype=F32), GRID_W)
    col = jnp.tile(jnp.arange(GRID_W, dtype=F32), rows)
    inv = jnp.power(ROPE_BASE, -jnp.arange(ROPE_PAIRS_PER_AXIS, dtype=F32) / ROPE_PAIRS_PER_AXIS)
    ang = jnp.concatenate([row[:, None] * inv, col[:, None] * inv], axis=-1)
    cos = jnp.cos(ang)
    sin = jnp.sin(ang)
    return jnp.tile(cos, (1, 4)), jnp.concatenate([-sin, -sin, sin, sin], axis=-1)


def _head_permutation():
    half = HEAD_DIM // 2
    base = jnp.concatenate([jnp.arange(0, half), jnp.arange(HEAD_DIM, HEAD_DIM + half),
                            jnp.arange(half, HEAD_DIM), jnp.arange(HEAD_DIM + half, HEAD_WIDTH)])
    return (jnp.arange(N_HEADS)[:, None] * HEAD_WIDTH + base[None, :]).reshape(-1)


def kernel(x, c, ctx, c_ctx, w_mod, b_mod, ln_g, ln_b, attn_w_qkv, attn_lambda, attn_subln_g,
           attn_w_o, pool_w, pool_scale, ffn_w_in, ffn_w_out):
    batch = x.shape[0]
    assert x.shape == (batch, SEQ, D_MODEL) and ctx.shape == (batch, CTX_LEN, D_MODEL)
    assert batch + 1 <= COND_ROWS

    cond = jnp.concatenate([c, c_ctx[None, :], jnp.zeros((COND_ROWS - batch - 1, D_MODEL), F32)], axis=0)
    m4 = _modulation(cond, w_mod, b_mod).reshape(DEPTH, COND_ROWS, 1, N_MOD * D_MODEL)

    perm = _head_permutation()
    w_qkv = attn_w_qkv[0]
    wq = (w_qkv[:, :D_MODEL][:, perm] * (HEAD_DIM ** -0.5 * math.log2(math.e))).astype(BF16)
    wk = w_qkv[:, D_MODEL:2 * D_MODEL][:, perm].astype(BF16)
    wv = w_qkv[:, 2 * D_MODEL:].astype(BF16)
    cos_t, sin_t = _rope_tables()
    q, k_all, vt_all = _qkv_project(x, ctx, m4, cos_t, sin_t, wq, wk, wv)
    o = _attention(q, k_all, vt_all, attn_lambda[0], attn_subln_g[0])
    x = _post(x, o, attn_w_o[0].astype(BF16), None, m4, 0, ln_g[0], ln_b[0],
              ffn_w_in[0].astype(BF16), ffn_w_out[0].astype(BF16))

    pooled = _pool(x, m4, 1)
    x = _post(x, pooled, pool_w[0].astype(BF16), pool_scale[0], m4, 1, ln_g[1], ln_b[1],
              ffn_w_in[1].astype(BF16), ffn_w_out[1].astype(BF16))
    return x
```
